```python
import math
import jax, jax.numpy as jnp
from jax import lax
import numpy as np

D_MODEL = 1024
BATCH = 32
SEQ = 2048
DEPTH = 2
DEC_BATCH = 8
DEC_SEQ = 2048
PAST_LEN = 128

N_META = 16
NORM_EPS = 1e-5

SSD_EXPAND = 2
D_INNER = SSD_EXPAND * D_MODEL
SSD_HEADDIM = 64
SSD_HEADS = D_INNER // SSD_HEADDIM
SSD_GROUPS = 4
SSD_STATE = 128
SSD_GN = SSD_GROUPS * SSD_STATE
D_XBC = D_INNER + 2 * SSD_GN
D_SSD_IN = D_INNER + D_XBC + 2 * SSD_HEADS
CONV_K = 5
CHUNK = 128

DA_HEADS = 8
DA_HEAD_DIM = D_MODEL // DA_HEADS // 2
DA_V_DIM = 2 * DA_HEAD_DIM
Q_BLOCK = 128
ROPE_THETA = 10000.0

PEER_HEADS = 8
N_KEYS = 128
N_EXPERTS = N_KEYS * N_KEYS
PEER_KEY_DIM = 128
PEER_HALF = PEER_KEY_DIM // 2
PEER_TOPK = 16
TOKEN_BLOCK = 512

N_MIXERS = 2
N_SSD_LAYERS = (DEPTH + 1) // 2
N_ATTN_LAYERS = DEPTH // 2

kernel_name = "hybrid_ssd_diffattn_peer_encoder"


def rmsnorm(x, w):
    xf = x.astype(jnp.float32)
    y = xf * lax.rsqrt(jnp.mean(xf * xf, axis=-1, keepdims=True) + NORM_EPS)
    return (y * w.astype(jnp.float32)).astype(x.dtype)


def group_rmsnorm_f32(y, w):
    shp = y.shape
    yg = y.reshape(shp[:-1] + (SSD_GROUPS, D_INNER // SSD_GROUPS))
    yg = yg * lax.rsqrt(jnp.mean(yg * yg, axis=-1, keepdims=True) + NORM_EPS)
    return yg.reshape(shp) * w.astype(jnp.float32)


def depthwise_conv_centred(x, w, b):
    pad = CONV_K // 2
    c = x.shape[-1]
    y = lax.conv_general_dilated(
        x, w.reshape(CONV_K, 1, c).astype(x.dtype), window_strides=(1,),
        padding=[(pad, pad)], dimension_numbers=("NWC", "WIO", "NWC"),
        feature_group_count=c)
    return y + b


def ssd_chunked_scan(x, dt, A, Bm, Cm):
    f32 = jnp.float32
    b, L, H, P = x.shape
    G, N = Bm.shape[2], Bm.shape[3]
    J = H // G
    c = L // CHUNK
    xg = (x * dt[..., None]).astype(f32).reshape(b, c, CHUNK, G, J, P)
    Bc = Bm.astype(f32).reshape(b, c, CHUNK, G, N)
    Cc = Cm.astype(f32).reshape(b, c, CHUNK, G, N)
    a = (dt.astype(f32) * A.astype(f32)).reshape(b, c, CHUNK, G, J)
    a_cs = jnp.cumsum(a, axis=2)
    seg = a_cs[:, :, :, None] - a_cs[:, :, None, :]
    causal = jnp.tril(jnp.ones((CHUNK, CHUNK), bool))[None, None, :, :, None, None]
    Lmat = jnp.exp(jnp.where(causal, seg, -jnp.inf))
    CB = jnp.einsum('bclgn,bcsgn->bclsg', Cc, Bc)
    y_diag = jnp.einsum('bclsg,bclsgj,bcsgjp->bclgjp', CB, Lmat, xg)
    decay_to_end = jnp.exp(a_cs[:, :, -1:] - a_cs)
    states = jnp.einsum('bclgn,bclgj,bclgjp->bcgjpn', Bc, decay_to_end, xg)
    chunk_decay = jnp.exp(a_cs[:, :, -1])

    def step(carry, inp):
        st, dec = inp
        return carry * dec[..., None, None] + st, carry

    init = jnp.zeros((b, G, J, P, N), f32)
    _, prev = lax.scan(step, init, (jnp.moveaxis(states, 1, 0), jnp.moveaxis(chunk_decay, 1, 0)))
    prev = jnp.moveaxis(prev, 0, 1)
    y_off = jnp.einsum('bclgn,bcgjpn,bclgj->bclgjp', Cc, prev, jnp.exp(a_cs))
    return (y_diag + y_off).reshape(b, L, H, P)


def ssd_mixer(h, in_proj, conv_w, conv_b, dt_bias_f, dt_bias_b, a_log_f, a_log_b, d_skip, gate_norm, out_proj):
    f32 = jnp.float32
    b, L, _ = h.shape
    proj = h @ in_proj
    z = proj[..., :D_INNER]
    xbc = proj[..., D_INNER:D_INNER + D_XBC]
    dt_raw = proj[..., D_INNER + D_XBC:]
    xbc = jax.nn.silu(depthwise_conv_centred(xbc, conv_w, conv_b))
    xs = xbc[..., :D_INNER].reshape(b, L, SSD_HEADS, SSD_HEADDIM)
    Bm = xbc[..., D_INNER:D_INNER + SSD_GN].reshape(b, L, SSD_GROUPS, SSD_STATE)
    Cm = xbc[..., D_INNER + SSD_GN:].reshape(b, L, SSD_GROUPS, SSD_STATE)
    dt_f = jax.nn.softplus(dt_raw[..., :SSD_HEADS] + dt_bias_f)
    dt_b = jax.nn.softplus(dt_raw[..., SSD_HEADS:] + dt_bias_b)
    pad = (-L) % CHUNK

    def front_pad(t):
        return jnp.pad(t, [(0, 0), (pad, 0)] + [(0, 0)] * (t.ndim - 2))

    def flip(t):
        return jnp.flip(t, axis=1)

    xp, Bp, Cp = front_pad(xs), front_pad(Bm), front_pad(Cm)
    y_fwd = ssd_chunked_scan(xp, front_pad(dt_f), -jnp.exp(a_log_f), Bp, Cp)
    y_bwd = flip(ssd_chunked_scan(flip(xp), flip(front_pad(dt_b)), -jnp.exp(a_log_b), flip(Bp), flip(Cp)))
    y = (y_fwd + y_bwd)[:, pad:] + xs.astype(f32) * d_skip.astype(f32)[:, None]
    y = y.reshape(b, L, D_INNER) * jax.nn.silu(z.astype(f32))
    y = group_rmsnorm_f32(y, gate_norm)
    return y.astype(h.dtype) @ out_proj


def apply_rope(x):
    L, d = x.shape[1], x.shape[-1]
    inv = 1.0 / (ROPE_THETA ** (jnp.arange(0, d, 2, dtype=jnp.float32) / d))
    ang = jnp.arange(L, dtype=jnp.float32)[:, None] * inv[None, :]
    cos = jnp.cos(ang)[None, :, None, :].astype(x.dtype)
    sin = jnp.sin(ang)[None, :, None, :].astype(x.dtype)
    x1, x2 = x[..., : d // 2], x[..., d // 2:]
    return jnp.concatenate([x1 * cos - x2 * sin, x2 * cos + x1 * sin], axis=-1)


def diff_attention(h, wqkv, lq1, lk1, lq2, lk2, subln, wo, layer_idx):
    b, L, _ = h.shape
    s_real = L - N_META
    qkv = h @ wqkv
    q = qkv[..., :D_MODEL].reshape(b, L, 2 * DA_HEADS, DA_HEAD_DIM)
    k = qkv[..., D_MODEL:2 * D_MODEL].reshape(b, L, 2 * DA_HEADS, DA_HEAD_DIM)
    v = qkv[..., 2 * D_MODEL:].reshape(b, L, DA_HEADS, DA_V_DIM)
    q = apply_rope(q) * (DA_HEAD_DIM ** -0.5)
    k = apply_rope(k)
    q = q.reshape(b, L, DA_HEADS, 2, DA_HEAD_DIM)
    k = k.reshape(b, L, DA_HEADS, 2, DA_HEAD_DIM)
    lam_init = 0.8 - 0.6 * math.exp(-0.3 * layer_idx)
    f32 = jnp.float32
    lam = (jnp.exp(jnp.sum(lq1.astype(f32) * lk1.astype(f32)))
           - jnp.exp(jnp.sum(lq2.astype(f32) * lk2.astype(f32))) + lam_init)

    def attend(qb):
        s = jnp.einsum('bqhmd,bkhmd->bhmqk', qb, k).astype(f32)
        p = jax.nn.softmax(s, axis=-1)
        w = (p[:, :, 0] - lam * p[:, :, 1]).astype(v.dtype)
        return jnp.einsum('bhqk,bkhe->bqhe', w, v)

    o_meta = attend(q[:, :N_META])
    nb = s_real // Q_BLOCK
    qr = q[:, N_META:].reshape(b, nb, Q_BLOCK, DA_HEADS, 2, DA_HEAD_DIM)
    o_real = lax.map(attend, jnp.moveaxis(qr, 1, 0))
    o_real = jnp.moveaxis(o_real, 0, 1).reshape(b, s_real, DA_HEADS, DA_V_DIM)
    o = jnp.concatenate([o_meta, o_real], axis=1)
    o = rmsnorm(o, subln) * (1.0 - lam_init)
    return o.reshape(b, L, D_MODEL) @ wo


def peer(h, wq, sub_keys, u, v):
    b, L, D = h.shape
    T = b * L
    nblk = -(-T // TOKEN_BLOCK)
    flat = jnp.pad(h.reshape(T, D), ((0, nblk * TOKEN_BLOCK - T), (0, 0))).reshape(nblk, TOKEN_BLOCK, D)

    def block(xb):
        q = (xb @ wq).reshape(TOKEN_BLOCK, PEER_HEADS, 2, PEER_HALF)
        s = jnp.einsum('thmc,hmkc->thmk', q, sub_keys).astype(jnp.float32)
        sv, si = lax.top_k(s, PEER_TOPK)
        cand = (sv[:, :, 0, :, None] + sv[:, :, 1, None, :]).reshape(TOKEN_BLOCK, PEER_HEADS, PEER_TOPK * PEER_TOPK)
        cand_idx = (si[:, :, 0, :, None] * N_KEYS + si[:, :, 1, None, :]).reshape(TOKEN_BLOCK, PEER_HEADS, PEER_TOPK * PEER_TOPK)
        top_s, top_pos = lax.top_k(cand, PEER_TOPK)
        eidx = jnp.take_along_axis(cand_idx, top_pos, axis=-1)
        g = jax.nn.softmax(top_s, axis=-1)
        act = jax.nn.gelu(jnp.einsum('td,thkd->thk', xb, u[eidx]), approximate=False)
        coef = (g * act).astype(xb.dtype)
        return jnp.einsum('thk,thkd->td', coef, v[eidx])

    out = lax.map(block, flat).reshape(nblk * TOKEN_BLOCK, D)[:T]
    return out.reshape(b, L, D)


def trunk(x, meta_tokens, p):
    b = x.shape[0]
    meta = jnp.broadcast_to(meta_tokens.astype(x.dtype)[None], (b, N_META, D_MODEL))
    h = jnp.concatenate([meta, x], axis=1)
    for i in range(DEPTH):
        hn = rmsnorm(h, p["norm_mix"][i])
        j = i // N_MIXERS
        if i % N_MIXERS == 0:
            h = h + ssd_mixer(hn, p["ssd_in_proj"][j], p["ssd_conv_w"][j], p["ssd_conv_b"][j],
                              p["ssd_dt_bias_f"][j], p["ssd_dt_bias_b"][j], p["ssd_a_log_f"][j],
                              p["ssd_a_log_b"][j], p["ssd_d"][j], p["ssd_gate_norm"][j], p["ssd_out_proj"][j])
        else:
            h = h + diff_attention(hn, p["attn_wqkv"][j], p["attn_lambda_q1"][j], p["attn_lambda_k1"][j],
                                   p["attn_lambda_q2"][j], p["attn_lambda_k2"][j], p["attn_subln"][j],
                                   p["attn_wo"][j], i)
        h = h + peer(rmsnorm(h, p["norm_ffn"][i]), p["peer_wq"][i], p["peer_keys"][i], p["peer_u"][i], p["peer_v"][i])
    h = rmsnorm(h, p["final_norm"])
    return h[:, N_META:]


def setup_inputs(seed: int = 0) -> dict:
    key = jax.random.key(seed)
    ks = jax.random.split(key, 32)
    f = jnp.float32
    nA, nB = N_SSD_LAYERS, N_ATTN_LAYERS

    def nrm(k, shape, scale):
        return jax.random.normal(k, shape, f) * scale

    def gain(k, shape):
        return 1.0 + 0.05 * jax.random.normal(k, shape, f)

    def dt_bias(k):
        dt0 = jnp.exp(jax.random.uniform(k, (nA, SSD_HEADS), f, math.log(1e-3), math.log(1e-1)))
        return dt0 + jnp.log(-jnp.expm1(-dt0))

    return {
        "x_prompt": nrm(ks[0], (BATCH, SEQ, D_MODEL), 1.0),
        "x_sample": nrm(ks[1], (DEC_BATCH, DEC_SEQ, D_MODEL), 1.0),
        "meta_tokens": nrm(ks[2], (N_META, D_MODEL), 1.0),
        "norm_mix": gain(ks[3], (DEPTH, D_MODEL)),
        "norm_ffn": gain(ks[4], (DEPTH, D_MODEL)),
        "ssd_in_proj": nrm(ks[5], (nA, D_MODEL, D_SSD_IN), D_MODEL ** -0.5),
        "ssd_conv_w": nrm(ks[6], (nA, CONV_K, D_XBC), CONV_K ** -0.5),
        "ssd_conv_b": nrm(ks[7], (nA, D_XBC), 0.02),
        "ssd_dt_bias_f": dt_bias(ks[8]),
        "ssd_dt_bias_b": dt_bias(ks[9]),
        "ssd_a_log_f": jnp.log(jax.random.uniform(ks[10], (nA, SSD_HEADS), f, 1.0, 16.0)),
        "ssd_a_log_b": jnp.log(jax.random.uniform(ks[11], (nA, SSD_HEADS), f, 1.0, 16.0)),
        "ssd_d": gain(ks[12], (nA, SSD_HEADS)),
        "ssd_gate_norm": gain(ks[13], (nA, D_INNER)),
        "ssd_out_proj": nrm(ks[14], (nA, D_INNER, D_MODEL), D_INNER ** -0.5),
        "attn_wqkv": nrm(ks[15], (nB, D_MODEL, 3 * D_MODEL), D_MODEL ** -0.5),
        "attn_lambda_q1": nrm(ks[16], (nB, DA_HEAD_DIM), 0.1),
        "attn_lambda_k1": nrm(ks[17], (nB, DA_HEAD_DIM), 0.1),
        "attn_lambda_q2": nrm(ks[18], (nB, DA_HEAD_DIM), 0.1),
        "attn_lambda_k2": nrm(ks[19], (nB, DA_HEAD_DIM), 0.1),
        "attn_subln": gain(ks[20], (nB, DA_V_DIM)),
        "attn_wo": nrm(ks[21], (nB, D_MODEL, D_MODEL), D_MODEL ** -0.5),
        "peer_wq": nrm(ks[22], (DEPTH, D_MODEL, PEER_HEADS * PEER_KEY_DIM), D_MODEL ** -0.5),
        "peer_keys": nrm(ks[23], (DEPTH, PEER_HEADS, 2, N_KEYS, PEER_HALF), PEER_HALF ** -0.5),
        "peer_u": nrm(ks[24], (DEPTH, N_EXPERTS, D_MODEL), D_MODEL ** -0.5),
        "peer_v": nrm(ks[25], (DEPTH, N_EXPERTS, D_MODEL), PEER_HEADS ** -0.5),
        "final_norm": gain(ks[26], (D_MODEL,)),
    }


def reference(x_prompt, x_sample, meta_tokens, norm_mix, norm_ffn, ssd_in_proj, ssd_conv_w, ssd_conv_b,
              ssd_dt_bias_f, ssd_dt_bias_b, ssd_a_log_f, ssd_a_log_b, ssd_d, ssd_gate_norm, ssd_out_proj,
              attn_wqkv, attn_lambda_q1, attn_lambda_k1, attn_lambda_q2, attn_lambda_k2, attn_subln, attn_wo,
              peer_wq, peer_keys, peer_u, peer_v, final_norm):
    params = {
        "norm_mix": norm_mix, "norm_ffn": norm_ffn,
        "ssd_in_proj": ssd_in_proj, "ssd_conv_w": ssd_conv_w, "ssd_conv_b": ssd_conv_b,
        "ssd_dt_bias_f": ssd_dt_bias_f, "ssd_dt_bias_b": ssd_dt_bias_b,
        "ssd_a_log_f": ssd_a_log_f, "ssd_a_log_b": ssd_a_log_b, "ssd_d": ssd_d,
        "ssd_gate_norm": ssd_gate_norm, "ssd_out_proj": ssd_out_proj,
        "attn_wqkv": attn_wqkv, "attn_lambda_q1": attn_lambda_q1, "attn_lambda_k1": attn_lambda_k1,
        "attn_lambda_q2": attn_lambda_q2, "attn_lambda_k2": attn_lambda_k2, "attn_subln": attn_subln,
        "attn_wo": attn_wo,
        "peer_wq": peer_wq, "peer_keys": peer_keys, "peer_u": peer_u, "peer_v": peer_v,
        "final_norm": final_norm,
    }
    y_prompt = trunk(x_prompt, meta_tokens, params)
    y_sample = trunk(x_sample, meta_tokens, params)
    return (y_prompt, y_sample)
```

```python
import functools
import math

import jax
import jax.numpy as jnp
from jax import lax
from jax.experimental import pallas as pl
from jax.experimental.pallas import tpu as pltpu

F32 = jnp.float32
BF16 = jnp.bfloat16

D_MODEL = 1024
N_META = 16
SEQ = 2048
SEQ_LEN = N_META + SEQ
NORM_EPS = 1e-5

D_INNER = 2048
SSD_HEADDIM = 64
SSD_HEADS = 32
SSD_GROUPS = 4
SSD_STATE = 128
HEADS_PER_GROUP = SSD_HEADS // SSD_GROUPS
GROUP_CH = D_INNER // SSD_GROUPS
SSD_GN = SSD_GROUPS * SSD_STATE
D_XBC = D_INNER + 2 * SSD_GN
CONV_K = 5
CHUNK = 128
FRONT_PAD = (-SEQ_LEN) % CHUNK
PADDED_LEN = SEQ_LEN + FRONT_PAD
N_CHUNKS = PADDED_LEN // CHUNK
GUARD = 8

DA_HEADS = 8
DA_HEAD_DIM = 64
DA_V_DIM = 128
ROPE_THETA = 10000.0
KEY_PAD_LEN = PADDED_LEN
Q_ROWS = 688

PEER_HEADS = 8
N_KEYS = 128
N_EXPERTS = N_KEYS * N_KEYS
PEER_HALF = 64
PEER_TOPK = 16

LANE = 128
TOKEN_TILE = 512
SCORE_TILE = 256
EXPERT_ROWS = 8
VMEM_LIMIT = 56 * 1024 * 1024


def _cparams(sem):
    return pltpu.CompilerParams(dimension_semantics=sem, vmem_limit_bytes=VMEM_LIMIT)


def _const_spec(shape):
    nd = len(shape)
    return pl.BlockSpec(shape, lambda *_: (0,) * nd, pipeline_mode=pl.Buffered(1))


def _swap_halves(x):
    n = x.shape[-1]
    lane = lax.broadcasted_iota(jnp.int32, x.shape, 1)
    return jnp.where((lane & 32) == 0, pltpu.roll(x, n - 32, 1), pltpu.roll(x, 32, 1))


def _proj_body(*refs, norm, residual, splits, n_rope, rope_scales):
    it = iter(refs)
    x_ref = next(it)
    g_ref = next(it) if norm else None
    w_ref = next(it)
    r_ref = next(it) if residual else None
    cos_ref, sin_ref = (next(it), next(it)) if n_rope else (None, None)
    outs = list(it)
    x = x_ref[...]
    if norm:
        xf = x.astype(F32)
        xf = xf * lax.rsqrt(jnp.mean(xf * xf, axis=-1, keepdims=True) + NORM_EPS) * g_ref[...]
        xb = xf.astype(BF16)
    else:
        xb = x.astype(BF16)
    for k, ((lo, hi), o_ref) in enumerate(zip(splits, outs)):
        acc = jnp.dot(xb, w_ref[:, lo:hi], preferred_element_type=F32)
        if k < n_rope:
            reps = (hi - lo) // LANE
            cos = jnp.concatenate([cos_ref[...]] * reps, axis=1)
            sin = jnp.concatenate([sin_ref[...]] * reps, axis=1)
            acc = (acc * cos + _swap_halves(acc) * sin) * rope_scales[k]
        if residual:
            acc = acc + r_ref[...]
        o_ref[...] = acc.astype(o_ref.dtype)


def _project(x, w_bf16, out_dtypes, splits, gain=None, residual=None, rope=None, rope_scales=(),
             tile=TOKEN_TILE):
    tp, k = x.shape
    n = w_bf16.shape[1]
    grid = (tp // tile,)
    args = [x]
    specs = [pl.BlockSpec((tile, k), lambda i: (i, 0))]
    if gain is not None:
        args.append(gain.reshape(1, k).astype(F32))
        specs.append(_const_spec((1, k)))
    args.append(w_bf16)
    specs.append(_const_spec((k, n)))
    if residual is not None:
        args.append(residual)
        specs.append(pl.BlockSpec((tile, n), lambda i: (i, 0)))
    if rope is not None:
        args += list(rope)
        specs += [pl.BlockSpec((tile, LANE), lambda i: (i, 0))] * 2
    out_shape = [jax.ShapeDtypeStruct((tp, hi - lo), dt) for (lo, hi), dt in zip(splits, out_dtypes)]
    out_specs = [pl.BlockSpec((tile, hi - lo), lambda i: (i, 0)) for lo, hi in splits]
    body = functools.partial(_proj_body, norm=gain is not None, residual=residual is not None,
                             splits=tuple(splits), n_rope=len(rope_scales), rope_scales=tuple(rope_scales))
    return pl.pallas_call(body, grid=grid, in_specs=specs, out_specs=out_specs, out_shape=out_shape,
                          compiler_params=_cparams(("parallel",)))(*args)


def _silu(x):
    return x * (1.0 / (1.0 + jnp.exp(-x)))


def _softplus(x):
    return jnp.maximum(x, 0.0) + jnp.log1p(jnp.exp(-jnp.abs(x)))


def _ssd_body(xs_ref, b_ref, c_ref, dt_ref, z_ref, cwx_ref, cwb_ref, cwc_ref, cbx_ref, cbb_ref, cbc_ref,
              dtb_ref, alog_ref, dsk_ref, gn_ref, out_ref, raw_s, x_s, bc_s, dt_s, y_s, st_s):
    n_in = GROUP_CH + 2 * SSD_STATE
    data0 = GUARD + FRONT_PAD
    raw_s[0:data0, :] = jnp.zeros((data0, n_in), F32)
    raw_s[data0 + SEQ_LEN:, :] = jnp.zeros((GUARD, n_in), F32)
    raw_s[data0:data0 + SEQ_LEN, 0:GROUP_CH] = xs_ref[...].astype(F32)
    raw_s[data0:data0 + SEQ_LEN, GROUP_CH:GROUP_CH + SSD_STATE] = b_ref[...].astype(F32)
    raw_s[data0:data0 + SEQ_LEN, GROUP_CH + SSD_STATE:] = c_ref[...].astype(F32)
    dt_s[0:FRONT_PAD, :] = jnp.zeros((FRONT_PAD, LANE), F32)
    dt_s[FRONT_PAD:, :] = _softplus(dt_ref[...] + dtb_ref[...])

    conv_w = jnp.concatenate([cwx_ref[...], cwb_ref[...], cwc_ref[...]], axis=1)
    conv_b = jnp.concatenate([cbx_ref[...], cbb_ref[...], cbc_ref[...]], axis=1)

    def conv_chunk(c, carry):
        r0 = pl.multiple_of(c * CHUNK, CHUNK)
        win = raw_s[pl.ds(r0, CHUNK + 2 * GUARD), :]
        acc = jnp.broadcast_to(conv_b, (CHUNK, n_in))
        for k in range(CONV_K):
            off = GUARD - CONV_K // 2 + k
            acc = acc + win[off:off + CHUNK, :] * conv_w[k:k + 1, :]
        row = lax.broadcasted_iota(jnp.int32, (CHUNK, n_in), 0) + r0
        act = jnp.where(row >= FRONT_PAD, _silu(acc), 0.0)
        x_s[pl.ds(r0, CHUNK), :] = act[:, 0:GROUP_CH]
        bc_s[pl.ds(r0, CHUNK), :] = act[:, GROUP_CH:]
        return carry

    lax.fori_loop(0, N_CHUNKS, conv_chunk, 0)

    a_row = -jnp.exp(alog_ref[...])
    li = lax.broadcasted_iota(jnp.int32, (CHUNK, CHUNK), 0)
    si = lax.broadcasted_iota(jnp.int32, (CHUNK, CHUNK), 1)

    def scan(direction):
        fwd = direction == 0
        keep = (si <= li) if fwd else (si >= li)
        tri = keep.astype(F32)
        end = CHUNK - 1 if fwd else 0
        st_s[...] = jnp.zeros(st_s.shape, F32)

        def chunk(ci, carry):
            c = ci if fwd else N_CHUNKS - 1 - ci
            r0 = pl.multiple_of(c * CHUNK, CHUNK)
            dtc = dt_s[pl.ds(r0, CHUNK), :]
            a = dtc * a_row
            a_cs = jnp.dot(tri, a, preferred_element_type=F32, precision=lax.Precision.HIGHEST)
            a_cs_t = a_cs.T
            dt_t = dtc.T
            bc = bc_s[pl.ds(r0, CHUNK), :]
            b_c = bc[:, 0:SSD_STATE]
            c_c = bc[:, SSD_STATE:]
            cb = lax.dot_general(c_c.astype(BF16), b_c.astype(BF16), (((1,), (1,)), ((), ())),
                                 preferred_element_type=F32)
            b_t = b_c.T
            xc = x_s[pl.ds(r0, CHUNK), :]
            ys = []
            for j in range(HEADS_PER_GROUP):
                lane = j if fwd else HEADS_PER_GROUP + j
                colb = jnp.broadcast_to(a_cs[:, lane:lane + 1], (CHUNK, CHUNK))
                rowb = a_cs_t[lane:lane + 1, :]
                dt_row = dt_t[lane:lane + 1, :]
                decay = jnp.exp(jnp.where(keep, colb - rowb, -jnp.inf))
                m = cb * decay * dt_row
                c_scaled = c_c * jnp.exp(colb)
                lhs = jnp.concatenate([m, c_scaled], axis=1).astype(BF16)
                xj = xc[:, j * SSD_HEADDIM:(j + 1) * SSD_HEADDIM]
                prev = st_s[j]
                rhs = jnp.concatenate([xj, prev], axis=0).astype(BF16)
                ys.append(jnp.dot(lhs, rhs, preferred_element_type=F32))
                total = a_cs_t[lane:lane + 1, end:end + 1]
                w_row = jnp.exp(total - rowb) * dt_row
                st_s[j] = prev * jnp.exp(total) + jnp.dot(
                    (b_t * w_row).astype(BF16), xj.astype(BF16), preferred_element_type=F32)
            y = jnp.concatenate(ys, axis=1)
            if fwd:
                y_s[pl.ds(r0, CHUNK), :] = y
            else:
                y_s[pl.ds(r0, CHUNK), :] += y
            return carry

        lax.fori_loop(0, N_CHUNKS, chunk, 0)

    scan(0)
    scan(1)

    d_skip = dsk_ref[...]
    gn = gn_ref[...]

    def finish(src0, dst0, rows):
        y = y_s[pl.ds(src0, rows), :] + x_s[pl.ds(src0, rows), :] * d_skip
        y = y * _silu(z_ref[pl.ds(dst0, rows), :].astype(F32))
        y = y * lax.rsqrt(jnp.mean(y * y, axis=-1, keepdims=True) + NORM_EPS) * gn
        out_ref[pl.ds(dst0, rows), :] = y.astype(out_ref.dtype)

    finish(FRONT_PAD, 0, N_META)

    def finish_chunk(c, carry):
        src0 = pl.multiple_of((c + 1) * CHUNK, CHUNK)
        dst0 = pl.multiple_of(c * CHUNK + N_META, N_META)
        finish(src0, dst0, CHUNK)
        return carry

    lax.fori_loop(0, N_CHUNKS - 1, finish_chunk, 0)


def _ssd_core(xbc, dt, z, conv_w, conv_b, dt_bias, a_log, d_skip, gate_norm, n_seq):
    tp = xbc.shape[0]
    g_x = GROUP_CH // LANE
    row = lambda b, g: (b, g)
    in_specs = [
        pl.BlockSpec((SEQ_LEN, GROUP_CH), row),
        pl.BlockSpec((SEQ_LEN, SSD_STATE), lambda b, g: (b, D_INNER // SSD_STATE + g)),
        pl.BlockSpec((SEQ_LEN, SSD_STATE), lambda b, g: (b, (D_INNER + SSD_GN) // SSD_STATE + g)),
        pl.BlockSpec((SEQ_LEN, LANE), row),
        pl.BlockSpec((SEQ_LEN, GROUP_CH), row),
        pl.BlockSpec((CONV_K, GROUP_CH), lambda b, g: (0, g)),
        pl.BlockSpec((CONV_K, SSD_STATE), lambda b, g: (0, D_INNER // SSD_STATE + g)),
        pl.BlockSpec((CONV_K, SSD_STATE), lambda b, g: (0, (D_INNER + SSD_GN) // SSD_STATE + g)),
        pl.BlockSpec((1, GROUP_CH), lambda b, g: (0, g)),
        pl.BlockSpec((1, SSD_STATE), lambda b, g: (0, D_INNER // SSD_STATE + g)),
        pl.BlockSpec((1, SSD_STATE), lambda b, g: (0, (D_INNER + SSD_GN) // SSD_STATE + g)),
        pl.BlockSpec((1, LANE), lambda b, g: (0, g)),
        pl.BlockSpec((1, LANE), lambda b, g: (0, g)),
        pl.BlockSpec((1, GROUP_CH), lambda b, g: (0, g)),
        pl.BlockSpec((1, GROUP_CH), lambda b, g: (0, g)),
    ]
    del g_x
    n_in = GROUP_CH + 2 * SSD_STATE
    scratch = [
        pltpu.VMEM((PADDED_LEN + 2 * GUARD, n_in), F32),
        pltpu.VMEM((PADDED_LEN, GROUP_CH), F32),
        pltpu.VMEM((PADDED_LEN, 2 * SSD_STATE), F32),
        pltpu.VMEM((PADDED_LEN, LANE), F32),
        pltpu.VMEM((PADDED_LEN, GROUP_CH), F32),
        pltpu.VMEM((HEADS_PER_GROUP, SSD_STATE, SSD_HEADDIM), F32),
    ]
    return pl.pallas_call(
        _ssd_body, grid=(n_seq, SSD_GROUPS), in_specs=in_specs,
        out_specs=pl.BlockSpec((SEQ_LEN, GROUP_CH), row),
        out_shape=jax.ShapeDtypeStruct((tp, D_INNER), BF16), scratch_shapes=scratch,
        compiler_params=_cparams(("parallel", "parallel")),
    )(xbc, xbc, xbc, dt, z, conv_w, conv_w, conv_w, conv_b, conv_b, conv_b, dt_bias, a_log, d_skip, gate_norm)


def _group_dt_columns(t):
    lead = t.shape[:-1]
    f = t[..., :SSD_HEADS].reshape(lead + (SSD_GROUPS, HEADS_PER_GROUP))
    b = t[..., SSD_HEADS:].reshape(lead + (SSD_GROUPS, HEADS_PER_GROUP))
    pad = jnp.zeros(lead + (SSD_GROUPS, LANE - 2 * HEADS_PER_GROUP), t.dtype)
    return jnp.concatenate([f, b, pad], axis=-1).reshape(lead + (SSD_GROUPS * LANE,))


def _ssd_layer(h, n_seq, gain, in_proj, conv_w, conv_b, dt_bias_f, dt_bias_b, a_log_f, a_log_b, d_skip,
               gate_norm, out_proj):
    w_dt = _group_dt_columns(in_proj[:, D_INNER + D_XBC:])
    w = jnp.concatenate([in_proj[:, :D_INNER + D_XBC], w_dt], axis=1).astype(BF16)
    n_dt = SSD_GROUPS * LANE
    splits = [(0, D_INNER), (D_INNER, D_INNER + D_XBC), (D_INNER + D_XBC, D_INNER + D_XBC + n_dt)]
    z, xbc, dt = _project(h, w, [BF16, BF16, F32], splits, gain=gain, tile=256)
    dt_bias = _group_dt_columns(jnp.concatenate([dt_bias_f, dt_bias_b])[None, :])
    a_log = _group_dt_columns(jnp.concatenate([a_log_f, a_log_b])[None, :])
    d_ch = jnp.repeat(d_skip, SSD_HEADDIM)[None, :]
    y = _ssd_core(xbc, dt, z, conv_w, conv_b[None, :], dt_bias, a_log, d_ch, gate_norm[None, :], n_seq)
    (h_new,) = _project(y, out_proj.astype(BF16), [F32], [(0, D_MODEL)], residual=h)
    return h_new


def _attn_body(q_ref, k_ref, v_ref, lq1_ref, lk1_ref, lq2_ref, lk2_ref, subln_ref, o_ref, k0_s, k1_s, v_s,
               *, lam_init):
    n_pad = KEY_PAD_LEN - SEQ_LEN
    width = 2 * DA_HEAD_DIM
    k = k_ref[...]
    lane = lax.broadcasted_iota(jnp.int32, (SEQ_LEN, width), 1)
    zero = jnp.zeros((), k.dtype)
    k0_s[0:SEQ_LEN, :] = jnp.where(lane < DA_HEAD_DIM, k, zero)
    k1_s[0:SEQ_LEN, :] = jnp.where(lane >= DA_HEAD_DIM, k, zero)
    v_s[0:SEQ_LEN, :] = v_ref[...]
    for ref in (k0_s, k1_s, v_s):
        ref[SEQ_LEN:, :] = jnp.zeros((n_pad, width), ref.dtype)
    lam = (jnp.exp(jnp.sum(lq1_ref[...] * lk1_ref[...], axis=-1, keepdims=True))
           - jnp.exp(jnp.sum(lq2_ref[...] * lk2_ref[...], axis=-1, keepdims=True)) + lam_init)
    col = lax.broadcasted_iota(jnp.int32, (1, KEY_PAD_LEN), 1)
    bias = jnp.where(col < SEQ_LEN, 0.0, -1e30).astype(F32)
    subln = subln_ref[...]

    def q_step(i, carry):
        r0 = pl.multiple_of(i * Q_ROWS, 16)
        q = q_ref[pl.ds(r0, Q_ROWS), :]
        parts = []
        for ks, coef in ((k0_s, None), (k1_s, lam)):
            s = lax.dot_general(q, ks[...], (((1,), (1,)), ((), ())), preferred_element_type=F32) + bias
            p = jnp.exp(s - jnp.max(s, axis=-1, keepdims=True))
            inv = 1.0 / jnp.sum(p, axis=-1, keepdims=True)
            parts.append(p * (inv if coef is None else inv * coef))
        w = (parts[0] - parts[1]).astype(BF16)
        o = jnp.dot(w, v_s[...], preferred_element_type=F32)
        o = o * lax.rsqrt(jnp.mean(o * o, axis=-1, keepdims=True) + NORM_EPS) * subln
        o_ref[pl.ds(r0, Q_ROWS), :] = (o * (1.0 - lam_init)).astype(o_ref.dtype)
        return carry

    lax.fori_loop(0, SEQ_LEN // Q_ROWS, q_step, 0)


def _attention_core(q, k, v, lq1, lk1, lq2, lk2, subln, n_seq, lam_init):
    tp = q.shape[0]
    width = 2 * DA_HEAD_DIM
    blk = pl.BlockSpec((SEQ_LEN, width), lambda b, h: (b, h))
    vec = lambda n: pl.BlockSpec((1, n), lambda b, h: (0, 0))
    lams = [t.reshape(1, DA_HEAD_DIM).astype(F32) for t in (lq1, lk1, lq2, lk2)]
    return pl.pallas_call(
        functools.partial(_attn_body, lam_init=lam_init), grid=(n_seq, DA_HEADS),
        in_specs=[blk, blk, blk] + [vec(DA_HEAD_DIM)] * 4 + [vec(DA_V_DIM)],
        out_specs=blk, out_shape=jax.ShapeDtypeStruct((tp, DA_HEADS * DA_V_DIM), BF16),
        scratch_shapes=[pltpu.VMEM((KEY_PAD_LEN, width), BF16)] * 3,
        compiler_params=_cparams(("parallel", "parallel")),
    )(q, k, v, *lams, subln.reshape(1, DA_V_DIM).astype(F32))


def _rope_tables(tp):
    pos = (jnp.arange(tp) % SEQ_LEN).astype(F32)
    inv = 1.0 / (ROPE_THETA ** (jnp.arange(0, DA_HEAD_DIM, 2, dtype=F32) / DA_HEAD_DIM))
    ang = pos[:, None] * inv[None, :]
    cos, sin = jnp.cos(ang), jnp.sin(ang)
    return jnp.concatenate([cos] * 4, axis=1), jnp.concatenate([-sin, sin] * 2, axis=1)


def _attention_layer(h, n_seq, gain, wqkv, lq1, lk1, lq2, lk2, subln, wo, layer_idx):
    tp = h.shape[0]
    splits = [(0, D_MODEL), (D_MODEL, 2 * D_MODEL), (2 * D_MODEL, 3 * D_MODEL)]
    q, k, v = _project(h, wqkv.astype(BF16), [BF16] * 3, splits, gain=gain, rope=_rope_tables(tp),
                       rope_scales=(DA_HEAD_DIM ** -0.5, 1.0))
    lam_init = 0.8 - 0.6 * math.exp(-0.3 * layer_idx)
    o = _attention_core(q, k, v, lq1, lk1, lq2, lk2, subln, n_seq, lam_init)
    (h_new,) = _project(o, wo.astype(BF16), [F32], [(0, D_MODEL)], residual=h)
    return h_new


N_COMBOS = 2 * PEER_HEADS


def _peer_score_body(h_ref, g_ref, wqt_ref, keys_ref, xnt_ref, theta_ref, wgt_ref, s2_ref, e2_ref,
                     st_s, tops_s):
    tb = h_ref.shape[0]
    x = h_ref[...]
    xn = x * lax.rsqrt(jnp.mean(x * x, axis=-1, keepdims=True) + NORM_EPS) * g_ref[...]
    xnt = xn.T.astype(BF16)
    xnt_ref[...] = xnt
    qt = jnp.dot(wqt_ref[...], xnt, preferred_element_type=F32).astype(BF16)
    for c in range(N_COMBOS):
        st_s[c] = jnp.dot(keys_ref[c], qt[c * PEER_HALF:(c + 1) * PEER_HALF, :], preferred_element_type=F32)

    for c in range(N_COMBOS):
        head, half = divmod(c, 2)

        def extract(rank, s, head=head, half=half):
            mx = jnp.max(s, axis=0, keepdims=True)
            tops_s[half, rank, head:head + 1, :] = mx
            return jnp.where(s == mx, -jnp.inf, s)

        lax.fori_loop(0, PEER_TOPK, extract, st_s[c])

    a = [tops_s[0, r] for r in range(PEER_TOPK)]
    b = [tops_s[1, r] for r in range(PEER_TOPK)]
    cands = [a[r1] + b[r2] for r1 in range(PEER_TOPK) for r2 in range(PEER_TOPK // (r1 + 1))]
    work = list(cands)
    count = jnp.zeros((PEER_HEADS, tb), F32)
    tau = cands[0]
    for _ in range(PEER_TOPK):
        mx = functools.reduce(jnp.maximum, work)
        hits = [w == mx for w in work]
        tau = jnp.where(count < PEER_TOPK, mx, tau)
        count = count + functools.reduce(jnp.add, [jnp.where(e, 1.0, 0.0) for e in hits])
        work = [jnp.where(e, -jnp.inf, w) for e, w in zip(hits, work)]
    top = cands[0]
    z = functools.reduce(jnp.add, [jnp.where(cd >= tau, jnp.exp(cd - top), 0.0) for cd in cands])
    inv_z = 1.0 / z

    for head in range(PEER_HEADS):
        s1 = st_s[2 * head]
        s2 = st_s[2 * head + 1]
        row = slice(head, head + 1)
        theta_ref[head] = tau[row, :] - s1
        wgt_ref[head] = jnp.exp(s1 - a[0][row, :]) * inv_z[row, :]
        s2_ref[head] = s2
        e2_ref[head] = jnp.exp(s2 - b[0][row, :])


def _gelu(x):
    return 0.5 * x * (1.0 + lax.erf(x * (2.0 ** -0.5)))


def _peer_dense_body(xnt_ref, u_ref, vt_ref, th_ref, wg_ref, s2_ref, e2_ref, h_ref, o_ref,
                     acc_s, act_s, coef_s):
    e = pl.program_id(1)
    tb = xnt_ref.shape[1]

    @pl.when(e == 0)
    def _():
        acc_s[...] = jnp.zeros(acc_s.shape, F32)

    act_s[...] = jnp.dot(u_ref[...], xnt_ref[...], preferred_element_type=F32)

    for i in range(EXPERT_ROWS):
        rows = slice(i * N_KEYS, (i + 1) * N_KEYS)

        def token_chunk(c, carry, i=i, rows=rows):
            lanes = pl.ds(pl.multiple_of(c * LANE, LANE), LANE)
            gate = jnp.zeros((N_KEYS, LANE), F32)
            for head in range(PEER_HEADS):
                theta = th_ref[head, i:i + 1, lanes]
                wgt = wg_ref[head, i:i + 1, lanes]
                gate = gate + jnp.where(s2_ref[head, :, lanes] >= theta, e2_ref[head, :, lanes], 0.0) * wgt
            coef_s[rows, lanes] = (gate * _gelu(act_s[rows, lanes])).astype(BF16)
            return carry

        lax.fori_loop(0, tb // LANE, token_chunk, 0)

    acc_s[...] += jnp.dot(vt_ref[...], coef_s[...], preferred_element_type=F32)

    @pl.when(e == pl.num_programs(1) - 1)
    def _():
        o_ref[...] = acc_s[...].T + h_ref[...]


def _peer_layer(h, gain, wq, sub_keys, u, v):
    tp = h.shape[0]
    n_q = PEER_HEADS * 2 * PEER_HALF
    keys = sub_keys.reshape(N_COMBOS, N_KEYS, PEER_HALF).astype(BF16)
    per_head = jax.ShapeDtypeStruct((PEER_HEADS, N_KEYS, tp), F32)
    head_blk = pl.BlockSpec((PEER_HEADS, N_KEYS, SCORE_TILE), lambda t: (0, 0, t))
    xnt, theta, wgt, s2, e2 = pl.pallas_call(
        _peer_score_body, grid=(tp // SCORE_TILE,),
        in_specs=[pl.BlockSpec((SCORE_TILE, D_MODEL), lambda t: (t, 0)), _const_spec((1, D_MODEL)),
                  _const_spec((n_q, D_MODEL)), _const_spec((N_COMBOS, N_KEYS, PEER_HALF))],
        out_specs=[pl.BlockSpec((D_MODEL, SCORE_TILE), lambda t: (0, t))] + [head_blk] * 4,
        out_shape=[jax.ShapeDtypeStruct((D_MODEL, tp), BF16)] + [per_head] * 4,
        scratch_shapes=[pltpu.VMEM((N_COMBOS, N_KEYS, SCORE_TILE), F32),
                        pltpu.VMEM((2, PEER_TOPK, PEER_HEADS, SCORE_TILE), F32)],
        compiler_params=_cparams(("parallel",)),
    )(h, gain.reshape(1, D_MODEL).astype(F32), wq.T.astype(BF16), keys)

    tb = TOKEN_TILE
    n_exp = EXPERT_ROWS * N_KEYS
    row_blk = pl.BlockSpec((PEER_HEADS, EXPERT_ROWS, tb), lambda t, e: (0, e, t))
    all_blk = pl.BlockSpec((PEER_HEADS, N_KEYS, tb), lambda t, e: (0, 0, t))
    tok_blk = pl.BlockSpec((tb, D_MODEL), lambda t, e: (t, 0))
    return pl.pallas_call(
        _peer_dense_body, grid=(tp // tb, N_EXPERTS // n_exp),
        in_specs=[pl.BlockSpec((D_MODEL, tb), lambda t, e: (0, t)),
                  pl.BlockSpec((n_exp, D_MODEL), lambda t, e: (e, 0)),
                  pl.BlockSpec((D_MODEL, n_exp), lambda t, e: (0, e)),
                  row_blk, row_blk, all_blk, all_blk, tok_blk],
        out_specs=tok_blk, out_shape=jax.ShapeDtypeStruct((tp, D_MODEL), F32),
        scratch_shapes=[pltpu.VMEM((D_MODEL, tb), F32), pltpu.VMEM((n_exp, tb), F32),
                        pltpu.VMEM((n_exp, tb), BF16)],
        compiler_params=_cparams(("parallel", "arbitrary")),
    )(xnt, u.astype(BF16), v.T.astype(BF16), theta, wgt, s2, e2, h)


def _final_body(h_ref, g_ref, o_ref):
    g = g_ref[...]

    def rows(c, carry):
        src = pl.multiple_of(c * CHUNK + N_META, N_META)
        x = h_ref[pl.ds(src, CHUNK), :]
        y = x * lax.rsqrt(jnp.mean(x * x, axis=-1, keepdims=True) + NORM_EPS) * g
        o_ref[pl.ds(pl.multiple_of(c * CHUNK, CHUNK), CHUNK), :] = y
        return carry

    lax.fori_loop(0, SEQ // CHUNK, rows, 0)


def _final_norm(h, gain, first_seq, n_seq):
    return pl.pallas_call(
        _final_body, grid=(n_seq,),
        in_specs=[pl.BlockSpec((SEQ_LEN, D_MODEL), lambda b: (b + first_seq, 0)), _const_spec((1, D_MODEL))],
        out_specs=pl.BlockSpec((None, SEQ, D_MODEL), lambda b: (b, 0, 0)),
        out_shape=jax.ShapeDtypeStruct((n_seq, SEQ, D_MODEL), F32),
        compiler_params=_cparams(("parallel",)),
    )(h, gain.reshape(1, D_MODEL).astype(F32))


def _trunk(x_all, meta_tokens, p):
    n_seq = x_all.shape[0]
    meta = jnp.broadcast_to(meta_tokens.astype(F32)[None], (n_seq, N_META, D_MODEL))
    h = jnp.concatenate([meta, x_all], axis=1).reshape(n_seq * SEQ_LEN, D_MODEL)
    tp = -(-h.shape[0] // TOKEN_TILE) * TOKEN_TILE
    h = jnp.pad(h, ((0, tp - h.shape[0]), (0, 0)))
    h = _ssd_layer(h, n_seq, p["norm_mix"][0], p["ssd_in_proj"][0], p["ssd_conv_w"][0], p["ssd_conv_b"][0],
                   p["ssd_dt_bias_f"][0], p["ssd_dt_bias_b"][0], p["ssd_a_log_f"][0], p["ssd_a_log_b"][0],
                   p["ssd_d"][0], p["ssd_gate_norm"][0], p["ssd_out_proj"][0])
    h = _peer_layer(h, p["norm_ffn"][0], p["peer_wq"][0], p["peer_keys"][0], p["peer_u"][0], p["peer_v"][0])
    h = _attention_layer(h, n_seq, p["norm_mix"][1], p["attn_wqkv"][0], p["attn_lambda_q1"][0],
                         p["attn_lambda_k1"][0], p["attn_lambda_q2"][0], p["attn_lambda_k2"][0],
                         p["attn_subln"][0], p["attn_wo"][0], 1)
    h = _peer_layer(h, p["norm_ffn"][1], p["peer_wq"][1], p["peer_keys"][1], p["peer_u"][1], p["peer_v"][1])
    return h


def kernel(x_prompt, x_sample, meta_tokens, norm_mix, norm_ffn, ssd_in_proj, ssd_conv_w, ssd_conv_b, ssd_dt_bias_f, ssd_dt_bias_b, ssd_a_log_f, ssd_a_log_b, ssd_d, ssd_gate_norm, ssd_out_proj, attn_wqkv, attn_lambda_q1, attn_lambda_k1, attn_lambda_q2, attn_lambda_k2, attn_subln, attn_wo, peer_wq, peer_keys, peer_u, peer_v, final_norm):
    p = {
        "norm_mix": norm_mix, "norm_ffn": norm_ffn,
        "ssd_in_proj": ssd_in_proj, "ssd_conv_w": ssd_conv_w, "ssd_conv_b": ssd_conv_b,
        "ssd_dt_bias_f": ssd_dt_bias_f, "ssd_dt_bias_b": ssd_dt_bias_b,
        "ssd_a_log_f": ssd_a_log_f, "ssd_a_log_b": ssd_a_log_b, "ssd_d": ssd_d,
        "ssd_gate_norm": ssd_gate_norm, "ssd_out_proj": ssd_out_proj,
        "attn_wqkv": attn_wqkv, "attn_lambda_q1": attn_lambda_q1, "attn_lambda_k1": attn_lambda_k1,
        "attn_lambda_q2": attn_lambda_q2, "attn_lambda_k2": attn_lambda_k2, "attn_subln": attn_subln,
        "attn_wo": attn_wo,
        "peer_wq": peer_wq, "peer_keys": peer_keys, "peer_u": peer_u, "peer_v": peer_v,
    }
    n_prompt = x_prompt.shape[0]
    h = _trunk(jnp.concatenate([x_prompt, x_sample], axis=0), meta_tokens, p)
    y_prompt = _final_norm(h, final_norm, 0, n_prompt)
    y_sample = _final_norm(h, final_norm, n_prompt, x_sample.shape[0])
    return (y_prompt, y_sample)
```
